```python
import math
import jax, jax.numpy as jnp
from jax import lax
import numpy as np

D_MODEL = 1024
BATCH = 16
SEQ = 256
DEPTH = 2
DEC_BATCH = 4
DEC_SEQ = 2048
PAST_LEN = 256

GRID_W = 64
HEAD_DIM = 64
GQA_Q_HEADS = 8
GQA_KV_HEADS = 2
MLA_HEADS = 8
MLA_Q_RANK = 384
MLA_KV_RANK = 256
MLA_NOPE = 64
MLA_ROPE = 32
MLA_V = 64
HY_CH = 512
HY_ORDER = 2
HY_BANDS = 8
HY_EMB = 1 + 2 * HY_BANDS
HY_FFN = 64
HY_TARGET = 1e-2
HY_FAST_DECAY_PCT = 0.3
HY_SLOW_DECAY_PCT = 1.5
N_BRANCH = 3
BRANCH_W = 512
FFN_DIM = 2816
ROPE_THETA = 10000.0
Q_BLOCK = 128
EPS = 1e-6
MOD_CHUNKS = 6
SPLIT_SIZES = (GQA_Q_HEADS * HEAD_DIM, GQA_KV_HEADS * HEAD_DIM, GQA_KV_HEADS * HEAD_DIM,
               MLA_Q_RANK, MLA_KV_RANK, MLA_ROPE, 3 * HY_CH, N_BRANCH * D_MODEL)
IN_COLS = sum(SPLIT_SIZES)

kernel_name = "hybrid_gqa_mla_hyena_prefix_diffusion_step"


def _split_points():
    pts, acc = [], 0
    for s in SPLIT_SIZES[:-1]:
        acc += s
        pts.append(acc)
    return pts


def rms_norm(x, g):
    xf = x.astype(jnp.float32)
    y = xf * lax.rsqrt(jnp.mean(xf * xf, axis=-1, keepdims=True) + EPS)
    return (y * g.astype(jnp.float32)).astype(x.dtype)


def grid_rope(L, rot_dim):
    rows = L // GRID_W
    row = jnp.repeat(jnp.arange(rows, dtype=jnp.float32), GRID_W)
    col = jnp.tile(jnp.arange(GRID_W, dtype=jnp.float32), rows)
    axis_dim = rot_dim // 2
    inv = ROPE_THETA ** (-jnp.arange(0, axis_dim, 2, dtype=jnp.float32) / axis_dim)
    ang = jnp.concatenate([row[:, None] * inv, col[:, None] * inv], axis=-1)
    return jnp.cos(ang), jnp.sin(ang)


def apply_rope(x, rope):
    cos, sin = rope
    shp = x.shape
    xr = x.astype(jnp.float32).reshape(shp[:-1] + (shp[-1] // 2, 2))
    x0, x1 = xr[..., 0], xr[..., 1]
    c = cos[:, None, :]
    s = sin[:, None, :]
    out = jnp.stack([x0 * c - x1 * s, x0 * s + x1 * c], axis=-1).reshape(shp)
    return out.astype(x.dtype)


def blocked_attention(q, k, v, scale):
    B, Lq, H, Dk = q.shape
    Hkv = k.shape[2]
    G = H // Hkv
    Dv = v.shape[-1]
    nb = Lq // Q_BLOCK
    qb = jnp.moveaxis(q.reshape(B, nb, Q_BLOCK, Hkv, G, Dk), 1, 0)

    def one_block(qblk):
        s = jnp.einsum("bqhgd,bkhd->bhgqk", qblk, k, preferred_element_type=jnp.float32) * scale
        p = jax.nn.softmax(s, axis=-1).astype(v.dtype)
        return jnp.einsum("bhgqk,bkhd->bqhgd", p, v)

    o = lax.map(one_block, qb)
    return jnp.moveaxis(o, 0, 1).reshape(B, Lq, H, Dv)


def dwconv3(x, w, b):
    xp = jnp.pad(x, ((0, 0), (1, 1), (0, 0)))
    return xp[:, :-2] * w[0] + xp[:, 1:-1] * w[1] + xp[:, 2:] * w[2] + b


def hyena_filters(L, w1, b1, w2, b2, w3, freq):
    f32 = jnp.float32
    t = jnp.arange(L, dtype=f32)
    tn = t / max(L - 1, 1)
    bands = jnp.linspace(1e-4, HY_BANDS - 1, HY_BANDS, dtype=f32)
    ang = (2.0 * math.pi / L) * t[:, None] * bands[None, :]
    z = jnp.concatenate([tn[:, None], jnp.cos(ang), -jnp.sin(ang)], axis=-1)
    freq = freq.astype(f32)
    h = jnp.sin(freq[0] * (z @ w1.astype(f32) + b1.astype(f32)))
    h = jnp.sin(freq[1] * (h @ w2.astype(f32) + b2.astype(f32)))
    h = (h @ w3.astype(f32)).reshape(L, 2, HY_ORDER, HY_CH)
    min_decay = math.log(HY_TARGET) / HY_FAST_DECAY_PCT
    max_decay = math.log(HY_TARGET) / HY_SLOW_DECAY_PCT
    deltas = jnp.abs(jnp.linspace(min_decay, max_decay, HY_CH, dtype=f32))
    window = jnp.exp(-tn[:, None] * deltas[None, :])
    h = h * window[:, None, None, :]
    fwd = h[:, 0]
    bwd = h[1:, 1]
    l1 = jnp.sum(jnp.abs(fwd), axis=0) + jnp.sum(jnp.abs(bwd), axis=0) + EPS
    fwd = fwd / l1
    bwd = bwd / l1
    kern = jnp.concatenate([fwd, jnp.zeros((1, HY_ORDER, HY_CH), f32), bwd[::-1]], axis=0)
    return jnp.fft.rfft(kern, axis=0)


def long_conv(u, kf):
    L = u.shape[1]
    uf = jnp.fft.rfft(u.astype(jnp.float32), n=2 * L, axis=1)
    return jnp.fft.irfft(uf * kf[None], n=2 * L, axis=1)[:, :L].astype(u.dtype)


def hyena_branch(hy_in, lp):
    L = hy_in.shape[1]
    u = dwconv3(hy_in, lp["hy_short_w"], lp["hy_short_b"])
    v, x1, x2 = jnp.split(u, 3, axis=-1)
    kf = hyena_filters(L, lp["hy_w1"], lp["hy_b1"], lp["hy_w2"], lp["hy_b2"], lp["hy_w3"], lp["hy_freq"])
    z = v
    for n, gate in enumerate((x1, x2)):
        z = gate * (long_conv(z, kf[:, n]) + lp["hy_bias"][n] * z)
    return z


def token_mixers(h, lp, ctx_cache, rope_a, rope_b):
    B, L, _ = h.shape
    proj = h @ lp["w_in"]
    q_a, k_a, v_a, cq, ckv, kpe, hy_in, gates = jnp.split(proj, _split_points(), axis=-1)
    q_a = rms_norm(q_a.reshape(B, L, GQA_Q_HEADS, HEAD_DIM), lp["gqa_q_norm"])
    k_a = rms_norm(k_a.reshape(B, L, GQA_KV_HEADS, HEAD_DIM), lp["gqa_k_norm"])
    v_a = v_a.reshape(B, L, GQA_KV_HEADS, HEAD_DIM)
    cq = rms_norm(cq, lp["mla_q_norm"])
    q_b = (cq @ lp["mla_w_uq"]).reshape(B, L, MLA_HEADS, MLA_NOPE + MLA_ROPE)
    q_nope, q_pe = q_b[..., :MLA_NOPE], q_b[..., MLA_NOPE:]
    ckv = rms_norm(ckv, lp["mla_kv_norm"])
    kpe = kpe[:, :, None, :]
    if ctx_cache is None:
        new_ctx = (k_a, v_a, ckv, kpe[:, :, 0])
        keys_a, vals_a, ckv_all, kpe_all = k_a, v_a, ckv, kpe
    else:
        new_ctx = None
        q_a = apply_rope(q_a, rope_a)
        q_pe = apply_rope(q_pe, rope_b)
        c_k, c_v, c_ckv, c_kpe = ctx_cache
        keys_a = jnp.concatenate([c_k, apply_rope(k_a, rope_a)], axis=1)
        vals_a = jnp.concatenate([c_v, v_a], axis=1)
        ckv_all = jnp.concatenate([c_ckv, ckv], axis=1)
        kpe_all = jnp.concatenate([c_kpe[:, :, None, :], apply_rope(kpe, rope_b)], axis=1)
    o_a = blocked_attention(q_a, keys_a, vals_a, HEAD_DIM ** -0.5).reshape(B, L, BRANCH_W)
    kv = (ckv_all @ lp["mla_w_ukv"]).reshape(B, -1, MLA_HEADS, MLA_NOPE + MLA_V)
    k_b = jnp.concatenate([kv[..., :MLA_NOPE], jnp.broadcast_to(kpe_all, kv.shape[:-1] + (MLA_ROPE,))], axis=-1)
    q_b = jnp.concatenate([q_nope, q_pe], axis=-1)
    o_b = blocked_attention(q_b, k_b, kv[..., MLA_NOPE:], (MLA_NOPE + MLA_ROPE) ** -0.5).reshape(B, L, BRANCH_W)
    o_c = hyena_branch(hy_in, lp)
    branches = jnp.einsum("nblc,ncd->nbld", jnp.stack([o_a, o_b, o_c]), lp["w_branch"])
    g = jax.nn.sigmoid(gates.reshape(B, L, N_BRANCH, D_MODEL))
    merged = jnp.einsum("blnd,nbld->bld", g, branches)
    return merged @ lp["w_out"], new_ctx


def conv_ffn(h, lp):
    u = dwconv3(h @ lp["ffn_up"], lp["ffn_conv_w"], lp["ffn_conv_b"])
    a, g = jnp.split(u, 2, axis=-1)
    return (jax.nn.silu(g) * a) @ lp["ffn_down"]


def trunk_layer(x, cond, lp, ctx_cache, rope_a, rope_b):
    mod = (jax.nn.silu(cond) @ lp["w_mod"] + lp["b_mod"])[:, None, :]
    sh1, sc1, g1, sh2, sc2, g2 = jnp.split(mod, MOD_CHUNKS, axis=-1)
    h = rms_norm(x, lp["norm1"]) * (1 + sc1) + sh1
    o, new_ctx = token_mixers(h, lp, ctx_cache, rope_a, rope_b)
    x = x + g1 * o
    h = rms_norm(x, lp["norm2"]) * (1 + sc2) + sh2
    x = x + g2 * conv_ffn(h, lp)
    return x, new_ctx


def setup_inputs(seed: int = 0) -> dict:
    key = jax.random.key(seed)
    keys = iter(jax.random.split(key, 48))
    f32 = jnp.float32
    D = D_MODEL

    def nrm(shape, scale):
        return jax.random.normal(next(keys), shape, f32) * scale

    def gain(shape):
        return 1.0 + nrm(shape, 0.05)

    return {
        "x_prompt": nrm((BATCH, SEQ, D), 1.0),
        "x_sample": nrm((DEC_BATCH, DEC_SEQ, D), 1.0),
        "cache_gqa_k": nrm((DEC_BATCH, DEPTH, PAST_LEN, GQA_KV_HEADS, HEAD_DIM), 1.0),
        "cache_gqa_v": nrm((DEC_BATCH, DEPTH, PAST_LEN, GQA_KV_HEADS, HEAD_DIM), 1.0),
        "cache_mla_ckv": nrm((DEC_BATCH, DEPTH, PAST_LEN, MLA_KV_RANK), 1.0),
        "cache_mla_kpe": nrm((DEC_BATCH, DEPTH, PAST_LEN, MLA_ROPE), 1.0),
        "c": nrm((DEC_BATCH, D), 1.0),
        "c_ctx": nrm((D,), 1.0),
        "w_mod": nrm((DEPTH, D, MOD_CHUNKS * D), 0.5 * D ** -0.5),
        "b_mod": nrm((DEPTH, MOD_CHUNKS * D), 0.01),
        "norm1": gain((DEPTH, D)),
        "norm2": gain((DEPTH, D)),
        "w_in": nrm((DEPTH, D, IN_COLS), D ** -0.5),
        "gqa_q_norm": gain((DEPTH, HEAD_DIM)),
        "gqa_k_norm": gain((DEPTH, HEAD_DIM)),
        "mla_q_norm": gain((DEPTH, MLA_Q_RANK)),
        "mla_kv_norm": gain((DEPTH, MLA_KV_RANK)),
        "mla_w_uq": nrm((DEPTH, MLA_Q_RANK, MLA_HEADS * (MLA_NOPE + MLA_ROPE)), MLA_Q_RANK ** -0.5),
        "mla_w_ukv": nrm((DEPTH, MLA_KV_RANK, MLA_HEADS * (MLA_NOPE + MLA_V)), MLA_KV_RANK ** -0.5),
        "hy_short_w": nrm((DEPTH, 3, 3 * HY_CH), 3 ** -0.5),
        "hy_short_b": nrm((DEPTH, 3 * HY_CH), 0.02),
        "hy_w1": nrm((DEPTH, HY_EMB, HY_FFN), HY_EMB ** -0.5),
        "hy_b1": nrm((DEPTH, HY_FFN), 0.1),
        "hy_w2": nrm((DEPTH, HY_FFN, HY_FFN), HY_FFN ** -0.5),
        "hy_b2": nrm((DEPTH, HY_FFN), 0.1),
        "hy_w3": nrm((DEPTH, HY_FFN, 2 * HY_ORDER * HY_CH), HY_FFN ** -0.5),
        "hy_freq": gain((DEPTH, 2, HY_FFN)),
        "hy_bias": nrm((DEPTH, HY_ORDER, HY_CH), 0.5),
        "w_branch": nrm((DEPTH, N_BRANCH, BRANCH_W, D), BRANCH_W ** -0.5),
        "w_out": nrm((DEPTH, D, D), D ** -0.5),
        "ffn_up": nrm((DEPTH, D, 2 * FFN_DIM), D ** -0.5),
        "ffn_conv_w": nrm((DEPTH, 3, 2 * FFN_DIM), 3 ** -0.5),
        "ffn_conv_b": nrm((DEPTH, 2 * FFN_DIM), 0.02),
        "ffn_down": nrm((DEPTH, FFN_DIM, D), FFN_DIM ** -0.5),
        "final_norm": gain((D,)),
    }


def reference(x_prompt, x_sample, cache_gqa_k, cache_gqa_v, cache_mla_ckv, cache_mla_kpe, c, c_ctx,
              w_mod, b_mod, norm1, norm2, w_in, gqa_q_norm, gqa_k_norm, mla_q_norm, mla_kv_norm,
              mla_w_uq, mla_w_ukv, hy_short_w, hy_short_b, hy_w1, hy_b1, hy_w2, hy_b2, hy_w3, hy_freq,
              hy_bias, w_branch, w_out, ffn_up, ffn_conv_w, ffn_conv_b, ffn_down, final_norm):
    L_lat = x_sample.shape[1]
    rope_a = grid_rope(L_lat, HEAD_DIM)
    rope_b = grid_rope(L_lat, MLA_ROPE)
    xp = x_prompt
    xs = x_sample
    ctx_cond = c_ctx[None, :]
    ks, vs, ckvs, kpes = [], [], [], []
    for l in range(DEPTH):
        lp = dict(w_mod=w_mod[l], b_mod=b_mod[l], norm1=norm1[l], norm2=norm2[l], w_in=w_in[l],
                  gqa_q_norm=gqa_q_norm[l], gqa_k_norm=gqa_k_norm[l], mla_q_norm=mla_q_norm[l],
                  mla_kv_norm=mla_kv_norm[l], mla_w_uq=mla_w_uq[l], mla_w_ukv=mla_w_ukv[l],
                  hy_short_w=hy_short_w[l], hy_short_b=hy_short_b[l], hy_w1=hy_w1[l], hy_b1=hy_b1[l],
                  hy_w2=hy_w2[l], hy_b2=hy_b2[l], hy_w3=hy_w3[l], hy_freq=hy_freq[l], hy_bias=hy_bias[l],
                  w_branch=w_branch[l], w_out=w_out[l], ffn_up=ffn_up[l], ffn_conv_w=ffn_conv_w[l],
                  ffn_conv_b=ffn_conv_b[l], ffn_down=ffn_down[l])
        xp, (k_l, v_l, ckv_l, kpe_l) = trunk_layer(xp, ctx_cond, lp, None, None, None)
        ks.append(k_l)
        vs.append(v_l)
        ckvs.append(ckv_l)
        kpes.append(kpe_l)
        cache_l = (cache_gqa_k[:, l], cache_gqa_v[:, l], cache_mla_ckv[:, l], cache_mla_kpe[:, l])
        xs, _ = trunk_layer(xs, c, lp, cache_l, rope_a, rope_b)
    y_prompt = rms_norm(xp, final_norm)
    y_sample = rms_norm(xs, final_norm)
    new_gqa_k = jnp.stack(ks, axis=1)
    new_gqa_v = jnp.stack(vs, axis=1)
    new_mla_ckv = jnp.stack(ckvs, axis=1)
    new_mla_kpe = jnp.stack(kpes, axis=1)
    return (y_prompt, y_sample, new_gqa_k, new_gqa_v, new_mla_ckv, new_mla_kpe)
```

```python
import functools
import math

import numpy as np
import jax
import jax.numpy as jnp
from jax import lax
from jax.experimental import pallas as pl
from jax.experimental.pallas import tpu as pltpu

D_MODEL = 1024
BATCH = 16
SEQ = 256
DEPTH = 2
DEC_BATCH = 4
DEC_SEQ = 2048
PAST_LEN = 256
GRID_W = 64
HEAD_DIM = 64
GQA_Q_HEADS = 8
GQA_KV_HEADS = 2
MLA_HEADS = 8
MLA_Q_RANK = 384
MLA_KV_RANK = 256
MLA_NOPE = 64
MLA_ROPE = 32
MLA_V = 64
HY_CH = 512
HY_ORDER = 2
HY_BANDS = 8
HY_EMB = 1 + 2 * HY_BANDS
HY_FFN = 64
HY_TARGET = 1e-2
HY_FAST_DECAY_PCT = 0.3
HY_SLOW_DECAY_PCT = 1.5
N_BRANCH = 3
BRANCH_W = 512
FFN_DIM = 2816
ROPE_THETA = 10000.0
EPS = 1e-6
MOD_CHUNKS = 6

F32 = jnp.float32
BF16 = jnp.bfloat16

LANES = 128
VMEM_LIMIT = 56 * 1024 * 1024
COND_ROWS = 8
TOK_TILE = 512
FFN_ROWS = 2048
FFN_COLS = 256
HY_COLS = 256
HY_ROWS = 512
DFT_ROWS = 64


def _dot(a, b):
    return jnp.dot(a, b, preferred_element_type=F32)


def _dot_t(a, b):
    return lax.dot_general(a, b, (((1,), (1,)), ((), ())), preferred_element_type=F32)


def _sigmoid(x):
    return 1.0 / (1.0 + jnp.exp(-x))


def _norm_mod(x, g, scale, shift):
    ms = jnp.mean(x * x, axis=-1, keepdims=True)
    return (x * lax.rsqrt(ms + EPS) * g) * (1.0 + scale) + shift


def _whole():
    return pl.BlockSpec(memory_space=pltpu.VMEM)


def _params(*sem):
    return pltpu.CompilerParams(dimension_semantics=sem, vmem_limit_bytes=VMEM_LIMIT)


def _rope_tables(L, rot_dim):
    rows = L // GRID_W
    row = np.repeat(np.arange(rows, dtype=np.float64), GRID_W)
    col = np.tile(np.arange(GRID_W, dtype=np.float64), rows)
    axis_dim = rot_dim // 2
    inv = ROPE_THETA ** (-np.arange(0, axis_dim, 2, dtype=np.float64) / axis_dim)
    ang = np.concatenate([row[:, None] * inv, col[:, None] * inv], axis=-1)
    cos = np.repeat(np.cos(ang), 2, axis=-1)
    sin = np.sin(ang)
    sin = np.stack([-sin, sin], axis=-1).reshape(L, rot_dim)
    reps = LANES // rot_dim
    return (jnp.asarray(np.tile(cos, (1, reps)), F32), jnp.asarray(np.tile(sin, (1, reps)), F32))


def _head_mean_matrix(width, head):
    idx = np.arange(width) // head
    return jnp.asarray((idx[:, None] == idx[None, :]).astype(np.float32) / head, BF16)


def _hyena_tables(L):
    t = np.arange(L, dtype=np.float64)
    tn = t / max(L - 1, 1)
    bands = np.linspace(1e-4, HY_BANDS - 1, HY_BANDS)
    ang = (2.0 * math.pi / L) * t[:, None] * bands[None, :]
    z = np.concatenate([tn[:, None], np.cos(ang), -np.sin(ang)], axis=-1)
    zp = np.zeros((L, LANES), np.float64)
    zp[:, :HY_EMB] = z
    min_decay = math.log(HY_TARGET) / HY_FAST_DECAY_PCT
    max_decay = math.log(HY_TARGET) / HY_SLOW_DECAY_PCT
    deltas = np.abs(np.linspace(min_decay, max_decay, HY_CH))
    window = np.exp(-tn[:, None] * deltas[None, :])
    return jnp.asarray(zp, F32), jnp.asarray(window, F32)


def _dft_seed_tables(L):
    N = 2 * L
    n = np.arange(L, dtype=np.int64)
    a = np.arange(L // DFT_ROWS, dtype=np.int64) * DFT_ROWS
    b = np.arange(DFT_ROWS, dtype=np.int64)
    ang_a = ((a[:, None] * n[None, :]) % N).astype(np.float64) * (2.0 * math.pi / N)
    ang_b = ((b[:, None] * n[None, :]) % N).astype(np.float64) * (2.0 * math.pi / N)
    f = lambda v: jnp.asarray(v, F32)
    return (f(np.cos(ang_a))[:, None, :], f(np.sin(ang_a))[:, None, :], f(np.cos(ang_b)), f(np.sin(ang_b)))


def _dft_kernel(ca_ref, sa_ref, cb_ref, sb_ref, c_ref, s_ref):
    ca, sa = ca_ref[...], sa_ref[...]
    cb, sb = cb_ref[...], sb_ref[...]
    c_ref[...] = (ca * cb - sa * sb).astype(BF16)
    s_ref[...] = (-(sa * cb + ca * sb)).astype(BF16)


def _dft_matrices(L):
    ca, sa, cb, sb = _dft_seed_tables(L)
    row = pl.BlockSpec((None, 1, L), lambda i: (i, 0, 0))
    fine = pl.BlockSpec((DFT_ROWS, L), lambda i: (0, 0))
    out = pl.BlockSpec((DFT_ROWS, L), lambda i: (i, 0))
    return pl.pallas_call(
        _dft_kernel, grid=(L // DFT_ROWS,), in_specs=[row, row, fine, fine], out_specs=[out, out],
        out_shape=[jax.ShapeDtypeStruct((L, L), BF16)] * 2, compiler_params=_params("parallel"),
        name=f"dft_tables_{L}")(ca, sa, cb, sb)


def _mod_kernel(c_ref, w_ref, b_ref, o_ref):
    c = c_ref[...]
    s = (c * _sigmoid(c)).astype(BF16)
    o_ref[...] = _dot(s, w_ref[...].astype(BF16)) + b_ref[...]


def _modulation(cond, w_mod, b_mod):
    tn = 512
    return pl.pallas_call(
        _mod_kernel, grid=(DEPTH, MOD_CHUNKS * D_MODEL // tn),
        in_specs=[pl.BlockSpec((COND_ROWS, D_MODEL), lambda l, j: (0, 0)),
                  pl.BlockSpec((None, D_MODEL, tn), lambda l, j: (l, 0, j)),
                  pl.BlockSpec((None, 1, tn), lambda l, j: (l, 0, j))],
        out_specs=pl.BlockSpec((None, COND_ROWS, tn), lambda l, j: (l, 0, j)),
        out_shape=jax.ShapeDtypeStruct((DEPTH, COND_ROWS, MOD_CHUNKS * D_MODEL), F32),
        compiler_params=_params("parallel", "parallel"), name="adaln_mod",
    )(cond, w_mod, b_mod.reshape(DEPTH, 1, MOD_CHUNKS * D_MODEL))


def _rope(x, cos, sin):
    lane = lax.broadcasted_iota(jnp.int32, cos.shape, 1)
    odd = (lane & 1) == 1
    parts = []
    for i in range(x.shape[1] // LANES):
        xc = x[:, i * LANES:(i + 1) * LANES]
        swapped = jnp.where(odd, pltpu.roll(xc, 1, axis=1), pltpu.roll(xc, LANES - 1, axis=1))
        parts.append(xc * cos + swapped * sin)
    return parts[0] if len(parts) == 1 else jnp.concatenate(parts, axis=1)


def _inproj_kernel(*refs, latent):
    it = iter(refs)
    x_ref, mod_ref, n1_ref = next(it), next(it), next(it)
    wq, wk, wv, wcq, wckv, wkpe, why, wuq, wukv = [next(it) for _ in range(9)]
    gq, gk, gcq, gckv, hm_q, hm_k = [next(it) for _ in range(6)]
    if latent:
        cos_a, sin_a, cos_b, sin_b = [next(it)[...] for _ in range(4)]
    qa_o, k2_o, v2_o, qn_o, qpe_o, kcat_o, vb_o, hy_o = [next(it) for _ in range(8)]

    h = _norm_mod(x_ref[...], n1_ref[...], mod_ref[1:2, :], mod_ref[0:1, :]).astype(BF16)

    def head_norm(y, hm_ref, g_ref):
        ms = _dot((y * y).astype(BF16), hm_ref[...])
        return y * lax.rsqrt(ms + EPS) * g_ref[...]

    qa = head_norm(_dot(h, wq[...]), hm_q, gq)
    k2 = head_norm(_dot(h, wk[...]), hm_k, gk)
    v2 = _dot(h, wv[...])
    if latent:
        qa = _rope(qa, cos_a, sin_a)
        k2r = _rope(k2, cos_a, sin_a)
    else:
        k2r = k2
    qa_o[...] = (qa * (HEAD_DIM ** -0.5)).astype(BF16)
    k2_o[...] = k2r.astype(BF16)
    v2_o[...] = v2.astype(BF16)

    cq = _dot(h, wcq[...])
    cq = cq * lax.rsqrt(jnp.mean(cq * cq, axis=-1, keepdims=True) + EPS) * gcq[...]
    qb = _dot(cq.astype(BF16), wuq[...]) * ((MLA_NOPE + MLA_ROPE) ** -0.5)
    nope_w = MLA_HEADS * MLA_NOPE
    qpe = qb[:, nope_w:]
    if latent:
        qpe = _rope(qpe, cos_b, sin_b)
    qn_o[...] = qb[:, :nope_w].astype(BF16)
    qpe_o[...] = qpe.astype(BF16)

    ckv = _dot(h, wckv[...])
    ckv = ckv * lax.rsqrt(jnp.mean(ckv * ckv, axis=-1, keepdims=True) + EPS) * gckv[...]
    kv = _dot(ckv.astype(BF16), wukv[...])
    kpe4 = _dot(h, wkpe[...])
    kpe4r = _rope(kpe4, cos_b, sin_b) if latent else kpe4
    kpe_b = kpe4r.astype(BF16)
    pieces = []
    for j in range(nope_w // LANES):
        pieces += [kv[:, j * LANES:(j + 1) * LANES].astype(BF16), kpe_b]
    kcat_o[...] = jnp.concatenate(pieces, axis=1)
    vb_o[...] = kv[:, nope_w:].astype(BF16)

    hy_o[...] = _dot(h, why[...]).astype(BF16)

    if not latent:
        ck_o, cv_o, cckv_o, ckpe_o = [next(it) for _ in range(4)]
        lo = lax.broadcasted_iota(jnp.int32, (k2.shape[0], LANES), 1) < HEAD_DIM
        ck_o[...] = jnp.where(lo, k2[:, :LANES], k2[:, LANES:])
        cv_o[...] = jnp.where(lo, v2[:, :LANES], v2[:, LANES:])
        cckv_o[...] = ckv
        ckpe_o[...] = kpe4[:, :MLA_ROPE]


def _in_projection(x, mod, norm1, wts, consts, *, latent, seq):
    rows = x.shape[0]
    tm = min(TOK_TILE, seq)
    per_seq = seq // tm
    if latent:
        mod_map = lambda t: (1 + t // per_seq, 0, 0)
    else:
        mod_map = lambda t: (0, 0, 0)
    tile = lambda w: pl.BlockSpec((tm, w), lambda t: (t, 0))
    in_specs = [tile(D_MODEL), pl.BlockSpec((None, MOD_CHUNKS, D_MODEL), mod_map), _whole()]
    in_specs += [_whole()] * 15
    args = [x, mod, norm1] + list(wts) + list(consts["gains"]) + [consts["hm_q"], consts["hm_k"]]
    if latent:
        rope_spec = pl.BlockSpec((tm, LANES), lambda t: (t % per_seq, 0))
        in_specs += [rope_spec] * 4
        args += list(consts["rope"])
    widths = [512, 256, 256, 512, 256, 1024, 512, 3 * HY_CH]
    out_specs = [tile(w) for w in widths]
    out_shape = [jax.ShapeDtypeStruct((rows, w), BF16) for w in widths]
    if not latent:
        cache_w = [128, 128, MLA_KV_RANK, MLA_ROPE]
        out_specs += [tile(w) for w in cache_w]
        out_shape += [jax.ShapeDtypeStruct((rows, w), F32) for w in cache_w]
    return pl.pallas_call(
        functools.partial(_inproj_kernel, latent=latent), grid=(rows // tm,),
        in_specs=in_specs, out_specs=out_specs, out_shape=out_shape,
        compiler_params=_params("parallel"), name="in_proj_lat" if latent else "in_proj_ctx")(*args)


def _cache_kv_kernel(ckv_ref, kpe_ref, wukv_ref, kcat_o, vb_o):
    kv = _dot(ckv_ref[...], wukv_ref[...])
    nope_w = MLA_HEADS * MLA_NOPE
    kpe = kpe_ref[...]
    pieces = []
    for j in range(nope_w // LANES):
        pieces += [kv[:, j * LANES:(j + 1) * LANES].astype(BF16), kpe]
    kcat_o[...] = jnp.concatenate(pieces, axis=1)
    vb_o[...] = kv[:, nope_w:].astype(BF16)


def _cache_kv(ckv, kpe4, wukv):
    rows = ckv.shape[0]
    tm = PAST_LEN
    tile = lambda w: pl.BlockSpec((tm, w), lambda t: (t, 0))
    return pl.pallas_call(
        _cache_kv_kernel, grid=(rows // tm,), in_specs=[tile(MLA_KV_RANK), tile(LANES), _whole()],
        out_specs=[tile(1024), tile(512)],
        out_shape=[jax.ShapeDtypeStruct((rows, 1024), BF16), jax.ShapeDtypeStruct((rows, 512), BF16)],
        compiler_params=_params("parallel"), name="cache_kv")(ckv, kpe4, wukv)


def _attn_kernel(*refs, mla, n_src):
    it = iter(refs)
    q_ref = next(it)
    qpe_ref = next(it) if mla else None
    srcs = [(next(it), next(it)) for _ in range(n_src)]
    o_ref = next(it)
    j = pl.program_id(1)
    tq = q_ref.shape[0]
    lane = lax.broadcasted_iota(jnp.int32, (tq, LANES), 1)
    lo = lane < HEAD_DIM
    q = q_ref[...]
    res = []
    for hh in range(2):
        keep = lo if hh == 0 else jnp.logical_not(lo)
        qh = q * keep.astype(F32).astype(BF16)
        if mla:
            pos = 2 * (j % 2) + hh
            keep_pe = jnp.right_shift(lane, 5) == pos
            qh = jnp.concatenate([qh, qpe_ref[...] * keep_pe.astype(F32).astype(BF16)], axis=1)
        scores = [_dot_t(qh, k_ref[...]) for k_ref, _ in srcs]
        m = jnp.max(scores[0], axis=-1, keepdims=True)
        for s in scores[1:]:
            m = jnp.maximum(m, jnp.max(s, axis=-1, keepdims=True))
        den = None
        acc = None
        for s, (_, v_ref) in zip(scores, srcs):
            p = jnp.exp(s - m)
            ps = jnp.sum(p, axis=-1, keepdims=True)
            pv = _dot(p.astype(BF16), v_ref[...])
            den = ps if den is None else den + ps
            acc = pv if acc is None else acc + pv
        res.append(acc / den)
    o_ref[...] = jnp.where(lo, res[0], res[1]).astype(o_ref.dtype)


def _attention(q, qpe, srcs, *, batch, seq, mla, tq):
    nq = seq // tq
    qmap = lambda b, j, i: (b * nq + i, j)
    in_specs = [pl.BlockSpec((tq, LANES), qmap)]
    args = [q]
    if mla:
        in_specs.append(pl.BlockSpec((tq, LANES), lambda b, j, i: (b * nq + i, j // 2)))
        args.append(qpe)
    for k, v, n in srcs:
        if mla:
            in_specs += [pl.BlockSpec((n, 2 * LANES), lambda b, j, i: (b, j)),
                         pl.BlockSpec((n, LANES), lambda b, j, i: (b, j))]
        else:
            in_specs += [pl.BlockSpec((n, LANES), lambda b, j, i: (b, j // 2)),
                         pl.BlockSpec((n, LANES), lambda b, j, i: (b, j // 2))]
        args += [k, v]
    return pl.pallas_call(
        functools.partial(_attn_kernel, mla=mla, n_src=len(srcs)), grid=(batch, 4, nq),
        in_specs=in_specs, out_specs=pl.BlockSpec((tq, LANES), qmap),
        out_shape=jax.ShapeDtypeStruct((batch * seq, 512), BF16),
        compiler_params=_params("parallel", "parallel", "parallel"),
        name=("mla" if mla else "gqa") + f"_attn_{seq}")(*args)


def _filter_kernel(z_ref, win_ref, w1_ref, b1_ref, w2_ref, b2_ref, fr_ref, w3f_ref, w3b_ref, c_ref, s_ref,
                   kre_o, kim_o, h2_s):
    L = z_ref.shape[0]
    first = (pl.program_id(0) == 0) & (pl.program_id(1) == 0)

    @pl.when(first)
    def _():
        hi = lax.Precision.HIGHEST
        h1 = jnp.sin(fr_ref[0:1, :] * (jnp.dot(z_ref[...], w1_ref[...], precision=hi,
                                               preferred_element_type=F32) + b1_ref[...]))
        h2_s[...] = jnp.sin(fr_ref[1:2, :] * (jnp.dot(h1, w2_ref[...], precision=hi,
                                                      preferred_element_type=F32) + b2_ref[...]))

    h2 = h2_s[...].astype(BF16)
    win = win_ref[...]
    row = lax.broadcasted_iota(jnp.int32, win.shape, 0)
    hf = _dot(h2, w3f_ref[...].astype(BF16)) * win
    hb = jnp.where(row == 0, 0.0, _dot(h2, w3b_ref[...].astype(BF16)) * win)
    l1 = jnp.sum(jnp.abs(hf), axis=0, keepdims=True) + jnp.sum(jnp.abs(hb), axis=0, keepdims=True) + EPS
    inv = 1.0 / l1
    even = (hf + hb) * inv
    oddp = (hf - hb) * inv
    sign = jnp.where((row & 1) == 1, -1.0, 1.0)
    nyq = jnp.sum(even * sign, axis=0, keepdims=True)
    kre = _dot(c_ref[...], even.astype(BF16))
    kim = _dot(s_ref[...], oddp.astype(BF16))
    n_inv = 1.0 / (2 * L)
    kre_o[...] = kre * jnp.where(row == 0, n_inv, 2.0 * n_inv)
    kim_o[...] = jnp.where(row == 0, nyq * n_inv, kim * (2.0 * n_inv))


def _hyena_filters(L, zemb, window, w1p, b1, w2, b2, freq, w3, cmat, smat):
    nct = HY_CH // HY_COLS
    per_dir = HY_ORDER * nct
    out = pl.BlockSpec((None, L, HY_COLS), lambda n, c: (n, 0, c))
    return pl.pallas_call(
        _filter_kernel, grid=(HY_ORDER, nct),
        in_specs=[_whole(), pl.BlockSpec((L, HY_COLS), lambda n, c: (0, c)),
                  _whole(), _whole(), _whole(), _whole(), _whole(),
                  pl.BlockSpec((HY_FFN, HY_COLS), lambda n, c: (0, n * nct + c)),
                  pl.BlockSpec((HY_FFN, HY_COLS), lambda n, c: (0, per_dir + n * nct + c)),
                  _whole(), _whole()],
        out_specs=[out, out], out_shape=[jax.ShapeDtypeStruct((HY_ORDER, L, HY_CH), F32)] * 2,
        scratch_shapes=[pltpu.VMEM((L, HY_FFN), F32)],
        compiler_params=_params("arbitrary", "arbitrary"), name=f"hyena_filters_{L}",
    )(zemb, window, w1p, b1, w2, b2, freq, w3, w3, cmat, smat)


def _conv3(x, w_ref, b_ref, first, last):
    n = x.shape[0]
    prev = jnp.where(first, 0.0, pltpu.roll(x, 1, axis=0))
    nxt = jnp.where(last, 0.0, pltpu.roll(x, n - 1, axis=0))
    return prev * w_ref[0:1, :] + x * w_ref[1:2, :] + nxt * w_ref[2:3, :] + b_ref[...]


def _hyena_kernel(v_ref, x1_ref, x2_ref, wv_ref, w1_ref, w2_ref, bv_ref, b1_ref, b2_ref,
                  kre_ref, kim_ref, bias_ref, c_ref, s_ref, o_ref, z_s, zb_s, g_s, yre_s, yim_s):
    L = v_ref.shape[0]
    rt = min(HY_ROWS, L)
    row = lax.broadcasted_iota(jnp.int32, (L, HY_COLS), 0)
    first, last = row == 0, row == L - 1
    z = _conv3(v_ref[...].astype(F32), wv_ref, bv_ref, first, last)
    z_s[...] = z
    zb_s[...] = z.astype(BF16)
    g_s[0] = _conv3(x1_ref[...].astype(F32), w1_ref, b1_ref, first, last)
    g_s[1] = _conv3(x2_ref[...].astype(F32), w2_ref, b2_ref, first, last)
    sign = jnp.where((lax.broadcasted_iota(jnp.int32, (rt, HY_COLS), 0) & 1) == 1, -1.0, 1.0)
    tiles = [pl.ds(t * rt, rt) for t in range(L // rt)]
    for n in range(HY_ORDER):
        nyq = None
        for rows in tiles:
            zb = zb_s[...]
            re = _dot(c_ref[rows, :], zb)
            im = _dot(s_ref[rows, :], zb)
            kre, kim = kre_ref[n, rows, :], kim_ref[n, rows, :]
            yre_s[rows, :] = (re * kre - im * kim).astype(BF16)
            yim_s[rows, :] = (re * kim + im * kre).astype(BF16)
            part = jnp.sum(z_s[rows, :] * sign, axis=0, keepdims=True)
            nyq = part if nyq is None else nyq + part
        nyq_y = nyq * kim_ref[n, 0:1, :]
        for rows in tiles:
            y = _dot(c_ref[rows, :], yre_s[...]) + _dot(s_ref[rows, :], yim_s[...]) + sign * nyq_y
            z = g_s[n, rows, :] * (y + bias_ref[n:n + 1, :] * z_s[rows, :])
            if n + 1 < HY_ORDER:
                z_s[rows, :] = z
                zb_s[rows, :] = z.astype(BF16)
            else:
                o_ref[rows, :] = z.astype(o_ref.dtype)


def _hyena(hy, short_w, short_b, kre, kim, bias, cmat, smat, *, batch, seq):
    nct = HY_CH // HY_COLS
    act = lambda part: pl.BlockSpec((seq, HY_COLS), lambda c, b: (b, part * nct + c))
    cw = lambda part: pl.BlockSpec((3, HY_COLS), lambda c, b: (0, part * nct + c))
    cb = lambda part: pl.BlockSpec((1, HY_COLS), lambda c, b: (0, part * nct + c))
    kf = pl.BlockSpec((HY_ORDER, seq, HY_COLS), lambda c, b: (0, 0, c), pipeline_mode=pl.Buffered(1))
    return pl.pallas_call(
        _hyena_kernel, grid=(nct, batch),
        in_specs=[act(0), act(1), act(2), cw(0), cw(1), cw(2), cb(0), cb(1), cb(2), kf, kf,
                  pl.BlockSpec((HY_ORDER, HY_COLS), lambda c, b: (0, c)), _whole(), _whole()],
        out_specs=pl.BlockSpec((seq, HY_COLS), lambda c, b: (b, c)),
        out_shape=jax.ShapeDtypeStruct((batch * seq, HY_CH), BF16),
        scratch_shapes=[pltpu.VMEM((seq, HY_COLS), F32), pltpu.VMEM((seq, HY_COLS), BF16),
                        pltpu.VMEM((HY_ORDER, seq, HY_COLS), F32),
                        pltpu.VMEM((seq, HY_COLS), BF16), pltpu.VMEM((seq, HY_COLS), BF16)],
        compiler_params=_params("parallel", "parallel"), name=f"hyena_{seq}",
    )(hy, hy, hy, short_w, short_w, short_w, short_b, short_b, short_b, kre, kim, bias, cmat, smat)


def _merge_kernel(x_ref, mod_ref, n1_ref, oa_ref, ob_ref, oc_ref, wg_ref, wb_ref, wo_ref, o_ref):
    x = x_ref[...]
    h = _norm_mod(x, n1_ref[...], mod_ref[1:2, :], mod_ref[0:1, :]).astype(BF16)
    merged = None
    for n, b_ref in enumerate((oa_ref, ob_ref, oc_ref)):
        gate = _sigmoid(_dot(h, wg_ref[:, n * D_MODEL:(n + 1) * D_MODEL]))
        term = gate * _dot(b_ref[...], wb_ref[n])
        merged = term if merged is None else merged + term
    o_ref[...] = x + mod_ref[2:3, :] * _dot(merged.astype(BF16), wo_ref[...])


def _merge(x, mod, norm1, oa, ob, oc, w_gate, w_branch, w_out, *, latent, seq):
    rows = x.shape[0]
    tm = min(TOK_TILE, seq)
    per_seq = seq // tm
    mod_map = (lambda t: (1 + t // per_seq, 0, 0)) if latent else (lambda t: (0, 0, 0))
    tile = lambda w: pl.BlockSpec((tm, w), lambda t: (t, 0))
    return pl.pallas_call(
        _merge_kernel, grid=(rows // tm,),
        in_specs=[tile(D_MODEL), pl.BlockSpec((None, MOD_CHUNKS, D_MODEL), mod_map), _whole(),
                  tile(BRANCH_W), tile(BRANCH_W), tile(BRANCH_W), _whole(), _whole(), _whole()],
        out_specs=tile(D_MODEL), out_shape=jax.ShapeDtypeStruct((rows, D_MODEL), F32),
        compiler_params=_params("parallel"), name="merge_lat" if latent else "merge_ctx",
    )(x, mod, norm1, oa, ob, oc, w_gate, w_branch, w_out)


def _ffn_kernel(x_ref, mod_ref, n2_ref, fin_ref, ua_ref, ug_ref, cwa_ref, cwg_ref, cba_ref, cbg_ref, dn_ref,
                o_ref, h_s, *, seq, final):
    j = pl.program_id(1)
    rows = x_ref.shape[0]

    @pl.when(j == 0)
    def _():
        h_s[...] = _norm_mod(x_ref[...], n2_ref[...], mod_ref[4:5, :], mod_ref[3:4, :]).astype(BF16)

    pos = lax.broadcasted_iota(jnp.int32, (rows, FFN_COLS), 0) & (seq - 1)
    first, last = pos == 0, pos == seq - 1
    h = h_s[...]
    a = _conv3(_dot(h, ua_ref[...]), cwa_ref, cba_ref, first, last)
    g = _conv3(_dot(h, ug_ref[...]), cwg_ref, cbg_ref, first, last)
    part = _dot((g * _sigmoid(g) * a).astype(BF16), dn_ref[...])

    @pl.when(j == 0)
    def _():
        o_ref[...] = part

    @pl.when(j > 0)
    def _():
        o_ref[...] += part

    @pl.when(j == pl.num_programs(1) - 1)
    def _():
        y = x_ref[...] + mod_ref[5:6, :] * o_ref[...]
        if final:
            ms = jnp.mean(y * y, axis=-1, keepdims=True)
            y = y * lax.rsqrt(ms + EPS) * fin_ref[...]
        o_ref[...] = y


def _conv_ffn(x, mod, norm2, final_g, up, conv_w, conv_b, down, *, latent, seq, final):
    rows = x.shape[0]
    tm = FFN_ROWS
    per_seq = max(seq // tm, 1)
    mod_map = (lambda t, j: (1 + t // per_seq, 0, 0)) if latent else (lambda t, j: (0, 0, 0))
    nj = FFN_DIM // FFN_COLS
    big = lambda: pl.BlockSpec((tm, D_MODEL), lambda t, j: (t, 0), pipeline_mode=pl.Buffered(1))
    col = lambda r, off: pl.BlockSpec((r, FFN_COLS), lambda t, j: (0, off + j))
    return pl.pallas_call(
        functools.partial(_ffn_kernel, seq=seq, final=final), grid=(rows // tm, nj),
        in_specs=[big(), pl.BlockSpec((None, MOD_CHUNKS, D_MODEL), mod_map), _whole(), _whole(),
                  col(D_MODEL, 0), col(D_MODEL, nj), col(3, 0), col(3, nj), col(1, 0), col(1, nj),
                  pl.BlockSpec((FFN_COLS, D_MODEL), lambda t, j: (j, 0))],
        out_specs=pl.BlockSpec((tm, D_MODEL), lambda t, j: (t, 0)),
        out_shape=jax.ShapeDtypeStruct((rows, D_MODEL), F32),
        scratch_shapes=[pltpu.VMEM((tm, D_MODEL), BF16)],
        compiler_params=_params("parallel", "arbitrary"), name="conv_ffn_lat" if latent else "conv_ffn_ctx",
    )(x, mod, norm2, final_g, up, up, conv_w, conv_w, conv_b, conv_b, down)


def _layer_weights(l, w_in, gqa_q_norm, gqa_k_norm, mla_q_norm, mla_kv_norm, mla_w_uq, mla_w_ukv):
    w = w_in[l]
    b = lambda a: a.astype(BF16)
    o = 0
    wq = w[:, o:o + 512]; o += 512
    wk = w[:, o:o + 128]; o += 128
    wv = w[:, o:o + 128]; o += 128
    wcq = w[:, o:o + MLA_Q_RANK]; o += MLA_Q_RANK
    wckv = w[:, o:o + MLA_KV_RANK]; o += MLA_KV_RANK
    wkpe = w[:, o:o + MLA_ROPE]; o += MLA_ROPE
    why = w[:, o:o + 3 * HY_CH]; o += 3 * HY_CH
    wgate = w[:, o:]
    dup = lambda a: jnp.concatenate([a[:, :64], a[:, :64], a[:, 64:], a[:, 64:]], axis=1)
    uq = mla_w_uq[l].reshape(MLA_Q_RANK, MLA_HEADS, MLA_NOPE + MLA_ROPE)
    wuq = jnp.concatenate([uq[:, :, :MLA_NOPE].reshape(MLA_Q_RANK, -1), uq[:, :, MLA_NOPE:].reshape(MLA_Q_RANK, -1)], axis=1)
    ukv = mla_w_ukv[l].reshape(MLA_KV_RANK, MLA_HEADS, MLA_NOPE + MLA_V)
    wukv = jnp.concatenate([ukv[:, :, :MLA_NOPE].reshape(MLA_KV_RANK, -1), ukv[:, :, MLA_NOPE:].reshape(MLA_KV_RANK, -1)], axis=1)
    wts = [b(wq), b(dup(wk)), b(dup(wv)), b(wcq), b(wckv), b(jnp.tile(wkpe, (1, 4))), b(why), b(wuq), b(wukv)]
    gains = [jnp.tile(gqa_q_norm[l], 8)[None, :], jnp.tile(gqa_k_norm[l], 4)[None, :],
             mla_q_norm[l][None, :], mla_kv_norm[l][None, :]]
    return wts, gains, b(wgate)


def kernel(x_prompt, x_sample, cache_gqa_k, cache_gqa_v, cache_mla_ckv, cache_mla_kpe, c, c_ctx, w_mod, b_mod, norm1, norm2, w_in, gqa_q_norm, gqa_k_norm, mla_q_norm, mla_kv_norm, mla_w_uq, mla_w_ukv, hy_short_w, hy_short_b, hy_w1, hy_b1, hy_w2, hy_b2, hy_w3, hy_freq, hy_bias, w_branch, w_out, ffn_up, ffn_conv_w, ffn_conv_b, ffn_down, final_norm):
    paths = {"ctx": dict(latent=False, batch=BATCH, seq=SEQ), "lat": dict(latent=True, batch=DEC_BATCH, seq=DEC_SEQ)}
    xs = {"ctx": x_prompt.reshape(BATCH * SEQ, D_MODEL), "lat": x_sample.reshape(DEC_BATCH * DEC_SEQ, D_MODEL)}

    cond = jnp.concatenate([c_ctx[None, :], c, jnp.zeros((COND_ROWS - 1 - DEC_BATCH, D_MODEL), F32)], axis=0)
    mod_all = _modulation(cond, w_mod, b_mod).reshape(DEPTH, COND_ROWS, MOD_CHUNKS, D_MODEL)

    hm_q, hm_k = _head_mean_matrix(512, HEAD_DIM), _head_mean_matrix(256, HEAD_DIM)
    rope = _rope_tables(DEC_SEQ, HEAD_DIM) + _rope_tables(DEC_SEQ, MLA_ROPE)
    dft = {name: _dft_matrices(p["seq"]) for name, p in paths.items()}
    hy_tab = {name: _hyena_tables(p["seq"]) for name, p in paths.items()}
    final_g = final_norm[None, :]

    new_k, new_v, new_ckv, new_kpe = [], [], [], []
    for l in range(DEPTH):
        mod = mod_all[l]
        wts, gains, w_gate = _layer_weights(l, w_in, gqa_q_norm, gqa_k_norm, mla_q_norm, mla_kv_norm, mla_w_uq, mla_w_ukv)
        consts = dict(gains=gains, hm_q=hm_q, hm_k=hm_k, rope=rope)
        n1, n2 = norm1[l][None, :], norm2[l][None, :]
        wb, wo = w_branch[l].astype(BF16), w_out[l].astype(BF16)
        up, down = ffn_up[l].astype(BF16), ffn_down[l].astype(BF16)
        w1p = jnp.zeros((LANES, HY_FFN), F32).at[:HY_EMB].set(hy_w1[l])

        ck = cache_gqa_k[:, l].reshape(DEC_BATCH * PAST_LEN, GQA_KV_HEADS, HEAD_DIM)
        cv = cache_gqa_v[:, l].reshape(DEC_BATCH * PAST_LEN, GQA_KV_HEADS, HEAD_DIM)
        dup = lambda a: jnp.concatenate([a[:, 0], a[:, 0], a[:, 1], a[:, 1]], axis=-1).astype(BF16)
        ck2, cv2 = dup(ck), dup(cv)
        cckv = cache_mla_ckv[:, l].reshape(DEC_BATCH * PAST_LEN, MLA_KV_RANK).astype(BF16)
        ckpe4 = jnp.tile(cache_mla_kpe[:, l].reshape(DEC_BATCH * PAST_LEN, MLA_ROPE), (1, 4)).astype(BF16)
        kcat_c, vb_c = _cache_kv(cckv, ckpe4, wts[8])

        for name, p in paths.items():
            latent, batch, seq = p["latent"], p["batch"], p["seq"]
            x = xs[name]
            outs = _in_projection(x, mod, n1, wts, consts, latent=latent, seq=seq)
            qa, k2, v2, qn, qpe, kcat, vb, hy = outs[:8]
            if not latent:
                new_k.append(outs[8]); new_v.append(outs[9]); new_ckv.append(outs[10]); new_kpe.append(outs[11])
            src_a = [(k2, v2, seq)]
            src_b = [(kcat, vb, seq)]
            if latent:
                src_a = [(ck2, cv2, PAST_LEN)] + src_a
                src_b = [(kcat_c, vb_c, PAST_LEN)] + src_b
            tq = min(256, seq)
            o_a = _attention(qa, None, src_a, batch=batch, seq=seq, mla=False, tq=tq)
            o_b = _attention(qn, qpe, src_b, batch=batch, seq=seq, mla=True, tq=tq)
            cmat, smat = dft[name]
            zemb, window = hy_tab[name]
            kre, kim = _hyena_filters(seq, zemb, window, w1p, hy_b1[l][None, :], hy_w2[l], hy_b2[l][None, :],
                                      hy_freq[l], hy_w3[l], cmat, smat)
            o_c = _hyena(hy, hy_short_w[l], hy_short_b[l][None, :], kre, kim, hy_bias[l], cmat, smat,
                         batch=batch, seq=seq)
            x = _merge(x, mod, n1, o_a, o_b, o_c, w_gate, wb, wo, latent=latent, seq=seq)
            xs[name] = _conv_ffn(x, mod, n2, final_g, up, ffn_conv_w[l], ffn_conv_b[l][None, :], down,
                                 latent=latent, seq=seq, final=(l == DEPTH - 1))

    y_prompt = xs["ctx"].reshape(BATCH, SEQ, D_MODEL)
    y_sample = xs["lat"].reshape(DEC_BATCH, DEC_SEQ, D_MODEL)
    stack = lambda parts, shape: jnp.stack([a.reshape(shape) for a in parts], axis=1)
    return (y_prompt, y_sample,
            stack(new_k, (BATCH, SEQ, GQA_KV_HEADS, HEAD_DIM)), stack(new_v, (BATCH, SEQ, GQA_KV_HEADS, HEAD_DIM)),
            stack(new_ckv, (BATCH, SEQ, MLA_KV_RANK)), stack(new_kpe, (BATCH, SEQ, MLA_ROPE)))
```

```python
import functools
import math

import numpy as np
import jax
import jax.numpy as jnp
from jax import lax
from jax.experimental import pallas as pl
from jax.experimental.pallas import tpu as pltpu

D_MODEL = 1024
BATCH = 16
SEQ = 256
DEPTH = 2
DEC_BATCH = 4
DEC_SEQ = 2048
PAST_LEN = 256
GRID_W = 64
HEAD_DIM = 64
GQA_Q_HEADS = 8
GQA_KV_HEADS = 2
MLA_HEADS = 8
MLA_Q_RANK = 384
MLA_KV_RANK = 256
MLA_NOPE = 64
MLA_ROPE = 32
MLA_V = 64
HY_CH = 512
HY_ORDER = 2
HY_BANDS = 8
HY_EMB = 1 + 2 * HY_BANDS
HY_FFN = 64
HY_TARGET = 1e-2
HY_FAST_DECAY_PCT = 0.3
HY_SLOW_DECAY_PCT = 1.5
N_BRANCH = 3
BRANCH_W = 512
FFN_DIM = 2816
ROPE_THETA = 10000.0
EPS = 1e-6
MOD_CHUNKS = 6

F32 = jnp.float32
BF16 = jnp.bfloat16

LANES = 128
VMEM_LIMIT = 56 * 1024 * 1024
COND_ROWS = 8
LOG2E = math.log2(math.e)
TOK_TILE = 512
ATTN_ROWS = 512
ATTN_CHUNK = 256
FFN_ROWS = 2048
FFN_COLS = 256
FFN_CHUNK = 256
FFN_PIECE = 32
SUBLANES = 8
HY_COLS = 256
HY_ROWS = 512
DFT_ROWS = 64


def _dot(a, b):
    return jnp.dot(a, b, preferred_element_type=F32)


def _dot_t(a, b):
    return lax.dot_general(a, b, (((1,), (1,)), ((), ())), preferred_element_type=F32)


def _sigmoid(x):
    return 1.0 / (1.0 + jnp.exp(-x))


def _norm_mod(x, g, scale, shift):
    ms = jnp.mean(x * x, axis=-1, keepdims=True)
    return (x * lax.rsqrt(ms + EPS) * g) * (1.0 + scale) + shift


def _whole():
    return pl.BlockSpec(memory_space=pltpu.VMEM)


def _params(*sem):
    return pltpu.CompilerParams(dimension_semantics=sem, vmem_limit_bytes=VMEM_LIMIT)


def _rope_tables(L, rot_dim):
    rows = L // GRID_W
    row = np.repeat(np.arange(rows, dtype=np.float64), GRID_W)
    col = np.tile(np.arange(GRID_W, dtype=np.float64), rows)
    axis_dim = rot_dim // 2
    inv = ROPE_THETA ** (-np.arange(0, axis_dim, 2, dtype=np.float64) / axis_dim)
    ang = np.concatenate([row[:, None] * inv, col[:, None] * inv], axis=-1)
    cos = np.repeat(np.cos(ang), 2, axis=-1)
    sin = np.sin(ang)
    sin = np.stack([-sin, sin], axis=-1).reshape(L, rot_dim)
    reps = LANES // rot_dim
    return (jnp.asarray(np.tile(cos, (1, reps)), F32), jnp.asarray(np.tile(sin, (1, reps)), F32))


def _head_mean_matrix(width, head):
    idx = np.arange(width) // head
    return jnp.asarray((idx[:, None] == idx[None, :]).astype(np.float32) / head, BF16)


def _hyena_tables(L):
    t = np.arange(L, dtype=np.float64)
    tn = t / max(L - 1, 1)
    bands = np.linspace(1e-4, HY_BANDS - 1, HY_BANDS)
    ang = (2.0 * math.pi / L) * t[:, None] * bands[None, :]
    z = np.concatenate([tn[:, None], np.cos(ang), -np.sin(ang)], axis=-1)
    zp = np.zeros((L, LANES), np.float64)
    zp[:, :HY_EMB] = z
    min_decay = math.log(HY_TARGET) / HY_FAST_DECAY_PCT
    max_decay = math.log(HY_TARGET) / HY_SLOW_DECAY_PCT
    deltas = np.abs(np.linspace(min_decay, max_decay, HY_CH))
    window = np.exp(-tn[:, None] * deltas[None, :])
    return jnp.asarray(zp, F32), jnp.asarray(window, F32)


def _dft_seed_tables(L):
    N = 2 * L
    n = np.arange(L, dtype=np.int64)
    a = np.arange(L // DFT_ROWS, dtype=np.int64) * DFT_ROWS
    b = np.arange(DFT_ROWS, dtype=np.int64)
    ang_a = ((a[:, None] * n[None, :]) % N).astype(np.float64) * (2.0 * math.pi / N)
    ang_b = ((b[:, None] * n[None, :]) % N).astype(np.float64) * (2.0 * math.pi / N)
    f = lambda v: jnp.asarray(v, F32)
    return (f(np.cos(ang_a))[:, None, :], f(np.sin(ang_a))[:, None, :], f(np.cos(ang_b)), f(np.sin(ang_b)))


def _dft_kernel(ca_ref, sa_ref, cb_ref, sb_ref, c_ref, s_ref):
    ca, sa = ca_ref[...], sa_ref[...]
    cb, sb = cb_ref[...], sb_ref[...]
    c_ref[...] = (ca * cb - sa * sb).astype(BF16)
    s_ref[...] = (-(sa * cb + ca * sb)).astype(BF16)


def _dft_matrices(L):
    ca, sa, cb, sb = _dft_seed_tables(L)
    row = pl.BlockSpec((None, 1, L), lambda i: (i, 0, 0))
    fine = pl.BlockSpec((DFT_ROWS, L), lambda i: (0, 0))
    out = pl.BlockSpec((DFT_ROWS, L), lambda i: (i, 0))
    return pl.pallas_call(
        _dft_kernel, grid=(L // DFT_ROWS,), in_specs=[row, row, fine, fine], out_specs=[out, out],
        out_shape=[jax.ShapeDtypeStruct((L, L), BF16)] * 2, compiler_params=_params("parallel"),
        name=f"dft_tables_{L}")(ca, sa, cb, sb)


def _mod_kernel(c_ref, w_ref, b_ref, o_ref):
    c = c_ref[...]
    s = (c * _sigmoid(c)).astype(BF16)
    o_ref[...] = _dot(s, w_ref[...].astype(BF16)) + b_ref[...]


def _modulation(cond, w_mod, b_mod):
    tn = 512
    return pl.pallas_call(
        _mod_kernel, grid=(DEPTH, MOD_CHUNKS * D_MODEL // tn),
        in_specs=[pl.BlockSpec((COND_ROWS, D_MODEL), lambda l, j: (0, 0)),
                  pl.BlockSpec((None, D_MODEL, tn), lambda l, j: (l, 0, j)),
                  pl.BlockSpec((None, 1, tn), lambda l, j: (l, 0, j))],
        out_specs=pl.BlockSpec((None, COND_ROWS, tn), lambda l, j: (l, 0, j)),
        out_shape=jax.ShapeDtypeStruct((DEPTH, COND_ROWS, MOD_CHUNKS * D_MODEL), F32),
        compiler_params=_params("parallel", "parallel"), name="adaln_mod",
    )(cond, w_mod, b_mod.reshape(DEPTH, 1, MOD_CHUNKS * D_MODEL))


def _rope(x, cos, sin):
    lane = lax.broadcasted_iota(jnp.int32, cos.shape, 1)
    odd = (lane & 1) == 1
    parts = []
    for i in range(x.shape[1] // LANES):
        xc = x[:, i * LANES:(i + 1) * LANES]
        swapped = jnp.where(odd, pltpu.roll(xc, 1, axis=1), pltpu.roll(xc, LANES - 1, axis=1))
        parts.append(xc * cos + swapped * sin)
    return parts[0] if len(parts) == 1 else jnp.concatenate(parts, axis=1)


def _inproj_kernel(*refs, latent):
    it = iter(refs)
    x_ref, mod_ref, n1_ref = next(it), next(it), next(it)
    wq, wk, wv, wcq, wckv, wkpe, why, wuq, wukv = [next(it) for _ in range(9)]
    gq, gk, gcq, gckv, hm_q, hm_k = [next(it) for _ in range(6)]
    if latent:
        cos_a, sin_a, cos_b, sin_b = [next(it)[...] for _ in range(4)]
    qa_o, k2_o, v2_o, qn_o, qpe_o, kcat_o, vb_o, hy_o = [next(it) for _ in range(8)]

    h = _norm_mod(x_ref[...], n1_ref[...], mod_ref[1:2, :], mod_ref[0:1, :]).astype(BF16)

    def head_norm(y, hm_ref, g_ref):
        ms = _dot((y * y).astype(BF16), hm_ref[...])
        return y * lax.rsqrt(ms + EPS) * g_ref[...]

    qa = head_norm(_dot(h, wq[...]), hm_q, gq)
    k2 = head_norm(_dot(h, wk[...]), hm_k, gk)
    v2 = _dot(h, wv[...])
    if latent:
        qa = _rope(qa, cos_a, sin_a)
        k2r = _rope(k2, cos_a, sin_a)
    else:
        k2r = k2
    qa_o[...] = (qa * (LOG2E * HEAD_DIM ** -0.5)).astype(BF16)
    k2_o[...] = k2r.astype(BF16)
    v2_o[...] = v2.astype(BF16)

    cq = _dot(h, wcq[...])
    cq = cq * lax.rsqrt(jnp.mean(cq * cq, axis=-1, keepdims=True) + EPS) * gcq[...]
    qb = _dot(cq.astype(BF16), wuq[...]) * (LOG2E * (MLA_NOPE + MLA_ROPE) ** -0.5)
    nope_w = MLA_HEADS * MLA_NOPE
    qpe = qb[:, nope_w:]
    if latent:
        qpe = _rope(qpe, cos_b, sin_b)
    qn_o[...] = qb[:, :nope_w].astype(BF16)
    qpe_o[...] = qpe.astype(BF16)

    ckv = _dot(h, wckv[...])
    ckv = ckv * lax.rsqrt(jnp.mean(ckv * ckv, axis=-1, keepdims=True) + EPS) * gckv[...]
    kv = _dot(ckv.astype(BF16), wukv[...])
    kpe4 = _dot(h, wkpe[...])
    kpe4r = _rope(kpe4, cos_b, sin_b) if latent else kpe4
    kpe_b = kpe4r.astype(BF16)
    pieces = []
    for j in range(nope_w // LANES):
        pieces += [kv[:, j * LANES:(j + 1) * LANES].astype(BF16), kpe_b]
    kcat_o[...] = jnp.concatenate(pieces, axis=1)
    vb_o[...] = kv[:, nope_w:].astype(BF16)

    hy_o[...] = _dot(h, why[...]).astype(BF16)

    if not latent:
        ck_o, cv_o, cckv_o, ckpe_o = [next(it) for _ in range(4)]
        lo = lax.broadcasted_iota(jnp.int32, (k2.shape[0], LANES), 1) < HEAD_DIM
        ck_o[...] = jnp.where(lo, k2[:, :LANES], k2[:, LANES:])
        cv_o[...] = jnp.where(lo, v2[:, :LANES], v2[:, LANES:])
        cckv_o[...] = ckv
        ckpe_o[...] = kpe4[:, :MLA_ROPE]


def _in_projection(x, mod, norm1, wts, consts, *, latent, seq):
    rows = x.shape[0]
    tm = min(TOK_TILE, seq)
    per_seq = seq // tm
    if latent:
        mod_map = lambda t: (1 + t // per_seq, 0, 0)
    else:
        mod_map = lambda t: (0, 0, 0)
    tile = lambda w: pl.BlockSpec((tm, w), lambda t: (t, 0))
    in_specs = [tile(D_MODEL), pl.BlockSpec((None, MOD_CHUNKS, D_MODEL), mod_map), _whole()]
    in_specs += [_whole()] * 15
    args = [x, mod, norm1] + list(wts) + list(consts["gains"]) + [consts["hm_q"], consts["hm_k"]]
    if latent:
        rope_spec = pl.BlockSpec((tm, LANES), lambda t: (t % per_seq, 0))
        in_specs += [rope_spec] * 4
        args += list(consts["rope"])
    widths = [512, 256, 256, 512, 256, 1024, 512, 3 * HY_CH]
    out_specs = [tile(w) for w in widths]
    out_shape = [jax.ShapeDtypeStruct((rows, w), BF16) for w in widths]
    if not latent:
        cache_w = [128, 128, MLA_KV_RANK, MLA_ROPE]
        out_specs += [tile(w) for w in cache_w]
        out_shape += [jax.ShapeDtypeStruct((rows, w), F32) for w in cache_w]
    return pl.pallas_call(
        functools.partial(_inproj_kernel, latent=latent), grid=(rows // tm,),
        in_specs=in_specs, out_specs=out_specs, out_shape=out_shape,
        compiler_params=_params("parallel"), name="in_proj_lat" if latent else "in_proj_ctx")(*args)


def _cache_kv_kernel(ckv_ref, kpe_ref, wukv_ref, kcat_o, vb_o):
    kv = _dot(ckv_ref[...], wukv_ref[...])
    nope_w = MLA_HEADS * MLA_NOPE
    kpe = kpe_ref[...]
    pieces = []
    for j in range(nope_w // LANES):
        pieces += [kv[:, j * LANES:(j + 1) * LANES].astype(BF16), kpe]
    kcat_o[...] = jnp.concatenate(pieces, axis=1)
    vb_o[...] = kv[:, nope_w:].astype(BF16)


def _cache_kv(ckv, kpe4, wukv):
    rows = ckv.shape[0]
    tm = PAST_LEN
    tile = lambda w: pl.BlockSpec((tm, w), lambda t: (t, 0))
    return pl.pallas_call(
        _cache_kv_kernel, grid=(rows // tm,), in_specs=[tile(MLA_KV_RANK), tile(LANES), _whole()],
        out_specs=[tile(1024), tile(512)],
        out_shape=[jax.ShapeDtypeStruct((rows, 1024), BF16), jax.ShapeDtypeStruct((rows, 512), BF16)],
        compiler_params=_params("parallel"), name="cache_kv")(ckv, kpe4, wukv)


def _attn_tick(slot, q_ref, qpe_ref, k_refs, v_refs, o_ref, s_refs, p_refs, den_s, pe_slot):
    other = 1 - slot
    tq = q_ref.shape[0]
    rc = min(ATTN_CHUNK, tq)
    lane = lax.broadcasted_iota(jnp.int32, (rc, LANES), 1)
    lo = lane < HEAD_DIM
    for r0 in range(0, tq, rc):
        rs = pl.ds(r0, rc)
        q = q_ref[rs, :]
        res = []
        for hh in range(2):
            acc = None
            for p_s, v_ref in zip(p_refs, v_refs):
                pv = _dot(p_s[slot, hh, rs, :], v_ref[...])
                acc = pv if acc is None else acc + pv
            res.append(acc / den_s[slot, hh, rs, :])

            m = None
            for s_s in s_refs:
                mm = jnp.max(s_s[other, hh, rs, :], axis=-1, keepdims=True)
                m = mm if m is None else jnp.maximum(m, mm)
            den = None
            for s_s, p_s in zip(s_refs, p_refs):
                p = jnp.exp2(s_s[other, hh, rs, :] - m)
                ps = jnp.sum(p, axis=-1, keepdims=True)
                den = ps if den is None else den + ps
                p_s[other, hh, rs, :] = p.astype(BF16)
            den_s[other, hh, rs, :] = jnp.broadcast_to(den, (rc, LANES))

            keep = lo if hh == 0 else jnp.logical_not(lo)
            qh = q * keep.astype(F32).astype(BF16)
            if qpe_ref is not None:
                keep_pe = jnp.right_shift(lane, 5) == pe_slot + hh
                qh = jnp.concatenate([qh, qpe_ref[rs, :] * keep_pe.astype(F32).astype(BF16)], axis=1)
            for s_s, k_ref in zip(s_refs, k_refs):
                s_s[slot, hh, rs, :] = _dot_t(qh, k_ref[...])
        o_ref[rs, :] = jnp.where(lo, res[0], res[1]).astype(o_ref.dtype)


def _attn_kernel(*refs, mla, n_src, nq, total):
    it = iter(refs)
    q_ref = next(it)
    qpe_ref = next(it) if mla else None
    k_refs = [next(it) for _ in range(n_src)]
    v_refs = [next(it) for _ in range(n_src)]
    o_ref = next(it)
    s_refs = [next(it) for _ in range(n_src)]
    p_refs = [next(it) for _ in range(n_src)]
    den_s = next(it)
    t = pl.program_id(0)
    j_qk = (jnp.minimum(t, total - 1) // nq) % 4
    pe_slot = 2 * (j_qk % 2)

    @pl.when(t == 0)
    def _():
        for r in s_refs + p_refs:
            r[...] = jnp.zeros_like(r)
        den_s[...] = jnp.ones_like(den_s)

    for slot in range(2):
        @pl.when(t % 2 == slot)
        def _(slot=slot):
            _attn_tick(slot, q_ref, qpe_ref, k_refs, v_refs, o_ref, s_refs, p_refs, den_s, pe_slot)


def _attention(q, qpe, srcs, *, batch, seq, mla, tq):
    nq = seq // tq
    total = batch * 4 * nq
    qk = lambda t: jnp.minimum(t, total - 1)
    pv = lambda t: jnp.clip(t - 2, 0, total - 1)
    b_of = lambda u: u // (4 * nq)
    j_of = lambda u: (u // nq) % 4
    row_of = lambda u: b_of(u) * nq + u % nq
    in_specs = [pl.BlockSpec((tq, LANES), lambda t: (row_of(qk(t)), j_of(qk(t))))]
    args = [q]
    if mla:
        in_specs.append(pl.BlockSpec((tq, LANES), lambda t: (row_of(qk(t)), j_of(qk(t)) // 2)))
        args.append(qpe)
    kcol = (lambda u: j_of(u)) if mla else (lambda u: j_of(u) // 2)
    kw = 2 * LANES if mla else LANES
    for k, _, n in srcs:
        in_specs.append(pl.BlockSpec((n, kw), lambda t: (b_of(qk(t)), kcol(qk(t)))))
        args.append(k)
    for _, v, n in srcs:
        in_specs.append(pl.BlockSpec((n, LANES), lambda t: (b_of(pv(t)), kcol(pv(t)))))
        args.append(v)
    scratch = [pltpu.VMEM((2, 2, tq, n), F32) for _, _, n in srcs]
    scratch += [pltpu.VMEM((2, 2, tq, n), BF16) for _, _, n in srcs]
    scratch.append(pltpu.VMEM((2, 2, tq, LANES), F32))
    return pl.pallas_call(
        functools.partial(_attn_kernel, mla=mla, n_src=len(srcs), nq=nq, total=total), grid=(total + 2,),
        in_specs=in_specs, out_specs=pl.BlockSpec((tq, LANES), lambda t: (row_of(pv(t)), j_of(pv(t)))),
        out_shape=jax.ShapeDtypeStruct((batch * seq, 512), BF16), scratch_shapes=scratch,
        compiler_params=_params("arbitrary"), name=("mla" if mla else "gqa") + f"_attn_{seq}")(*args)


def _filter_kernel(z_ref, win_ref, w1_ref, b1_ref, w2_ref, b2_ref, fr_ref, w3f_ref, w3b_ref, c_ref, s_ref,
                   kre_o, kim_o, h2_s):
    L = z_ref.shape[0]
    first = (pl.program_id(0) == 0) & (pl.program_id(1) == 0)

    @pl.when(first)
    def _():
        hi = lax.Precision.HIGHEST
        h1 = jnp.sin(fr_ref[0:1, :] * (jnp.dot(z_ref[...], w1_ref[...], precision=hi,
                                               preferred_element_type=F32) + b1_ref[...]))
        h2_s[...] = jnp.sin(fr_ref[1:2, :] * (jnp.dot(h1, w2_ref[...], precision=hi,
                                                      preferred_element_type=F32) + b2_ref[...]))

    h2 = h2_s[...].astype(BF16)
    win = win_ref[...]
    row = lax.broadcasted_iota(jnp.int32, win.shape, 0)
    hf = _dot(h2, w3f_ref[...].astype(BF16)) * win
    hb = jnp.where(row == 0, 0.0, _dot(h2, w3b_ref[...].astype(BF16)) * win)
    l1 = jnp.sum(jnp.abs(hf), axis=0, keepdims=True) + jnp.sum(jnp.abs(hb), axis=0, keepdims=True) + EPS
    inv = 1.0 / l1
    even = (hf + hb) * inv
    oddp = (hf - hb) * inv
    sign = jnp.where((row & 1) == 1, -1.0, 1.0)
    nyq = jnp.sum(even * sign, axis=0, keepdims=True)
    kre = _dot(c_ref[...], even.astype(BF16))
    kim = _dot(s_ref[...], oddp.astype(BF16))
    n_inv = 1.0 / (2 * L)
    kre_o[...] = kre * jnp.where(row == 0, n_inv, 2.0 * n_inv)
    kim_o[...] = jnp.where(row == 0, nyq * n_inv, kim * (2.0 * n_inv))


def _hyena_filters(L, zemb, window, w1p, b1, w2, b2, freq, w3, cmat, smat):
    nct = HY_CH // HY_COLS
    per_dir = HY_ORDER * nct
    out = pl.BlockSpec((None, L, HY_COLS), lambda n, c: (n, 0, c))
    return pl.pallas_call(
        _filter_kernel, grid=(HY_ORDER, nct),
        in_specs=[_whole(), pl.BlockSpec((L, HY_COLS), lambda n, c: (0, c)),
                  _whole(), _whole(), _whole(), _whole(), _whole(),
                  pl.BlockSpec((HY_FFN, HY_COLS), lambda n, c: (0, n * nct + c)),
                  pl.BlockSpec((HY_FFN, HY_COLS), lambda n, c: (0, per_dir + n * nct + c)),
                  _whole(), _whole()],
        out_specs=[out, out], out_shape=[jax.ShapeDtypeStruct((HY_ORDER, L, HY_CH), F32)] * 2,
        scratch_shapes=[pltpu.VMEM((L, HY_FFN), F32)],
        compiler_params=_params("arbitrary", "arbitrary"), name=f"hyena_filters_{L}",
    )(zemb, window, w1p, b1, w2, b2, freq, w3, w3, cmat, smat)


def _conv3(x, w_ref, b_ref, first, last):
    n = x.shape[0]
    prev = jnp.where(first, 0.0, pltpu.roll(x, 1, axis=0))
    nxt = jnp.where(last, 0.0, pltpu.roll(x, n - 1, axis=0))
    return prev * w_ref[0:1, :] + x * w_ref[1:2, :] + nxt * w_ref[2:3, :] + b_ref[...]


def _hyena_kernel(v_ref, x1_ref, x2_ref, wv_ref, w1_ref, w2_ref, bv_ref, b1_ref, b2_ref,
                  kre_ref, kim_ref, bias_ref, c_ref, s_ref, o_ref, z_s, zb_s, g_s, yre_s, yim_s):
    L = v_ref.shape[0]
    rt = min(HY_ROWS, L)
    row = lax.broadcasted_iota(jnp.int32, (L, HY_COLS), 0)
    first, last = row == 0, row == L - 1
    z = _conv3(v_ref[...].astype(F32), wv_ref, bv_ref, first, last)
    z_s[...] = z
    zb_s[...] = z.astype(BF16)
    g_s[0] = _conv3(x1_ref[...].astype(F32), w1_ref, b1_ref, first, last)
    g_s[1] = _conv3(x2_ref[...].astype(F32), w2_ref, b2_ref, first, last)
    sign = jnp.where((lax.broadcasted_iota(jnp.int32, (rt, HY_COLS), 0) & 1) == 1, -1.0, 1.0)
    tiles = [pl.ds(t * rt, rt) for t in range(L // rt)]
    for n in range(HY_ORDER):
        nyq = None
        for rows in tiles:
            zb = zb_s[...]
            re = _dot(c_ref[rows, :], zb)
            im = _dot(s_ref[rows, :], zb)
            kre, kim = kre_ref[n, rows, :], kim_ref[n, rows, :]
            yre_s[rows, :] = (re * kre - im * kim).astype(BF16)
            yim_s[rows, :] = (re * kim + im * kre).astype(BF16)
            part = jnp.sum(z_s[rows, :] * sign, axis=0, keepdims=True)
            nyq = part if nyq is None else nyq + part
        nyq_y = nyq * kim_ref[n, 0:1, :]
        for rows in tiles:
            y = _dot(c_ref[rows, :], yre_s[...]) + _dot(s_ref[rows, :], yim_s[...]) + sign * nyq_y
            z = g_s[n, rows, :] * (y + bias_ref[n:n + 1, :] * z_s[rows, :])
            if n + 1 < HY_ORDER:
                z_s[rows, :] = z
                zb_s[rows, :] = z.astype(BF16)
            else:
                o_ref[rows, :] = z.astype(o_ref.dtype)


def _hyena(hy, short_w, short_b, kre, kim, bias, cmat, smat, *, batch, seq):
    nct = HY_CH // HY_COLS
    act = lambda part: pl.BlockSpec((seq, HY_COLS), lambda c, b: (b, part * nct + c))
    cw = lambda part: pl.BlockSpec((3, HY_COLS), lambda c, b: (0, part * nct + c))
    cb = lambda part: pl.BlockSpec((1, HY_COLS), lambda c, b: (0, part * nct + c))
    kf = pl.BlockSpec((HY_ORDER, seq, HY_COLS), lambda c, b: (0, 0, c), pipeline_mode=pl.Buffered(1))
    return pl.pallas_call(
        _hyena_kernel, grid=(nct, batch),
        in_specs=[act(0), act(1), act(2), cw(0), cw(1), cw(2), cb(0), cb(1), cb(2), kf, kf,
                  pl.BlockSpec((HY_ORDER, HY_COLS), lambda c, b: (0, c)), _whole(), _whole()],
        out_specs=pl.BlockSpec((seq, HY_COLS), lambda c, b: (b, c)),
        out_shape=jax.ShapeDtypeStruct((batch * seq, HY_CH), BF16),
        scratch_shapes=[pltpu.VMEM((seq, HY_COLS), F32), pltpu.VMEM((seq, HY_COLS), BF16),
                        pltpu.VMEM((HY_ORDER, seq, HY_COLS), F32),
                        pltpu.VMEM((seq, HY_COLS), BF16), pltpu.VMEM((seq, HY_COLS), BF16)],
        compiler_params=_params("parallel", "parallel"), name=f"hyena_{seq}",
    )(hy, hy, hy, short_w, short_w, short_w, short_b, short_b, short_b, kre, kim, bias, cmat, smat)


def _merge_kernel(x_ref, mod_ref, n1_ref, n2_ref, oa_ref, ob_ref, oc_ref, wg_ref, wb_ref, wo_ref, o_ref, h2_ref):
    x = x_ref[...]
    h = _norm_mod(x, n1_ref[...], mod_ref[1:2, :], mod_ref[0:1, :]).astype(BF16)
    merged = None
    for n, b_ref in enumerate((oa_ref, ob_ref, oc_ref)):
        gate = _sigmoid(_dot(h, wg_ref[:, n * D_MODEL:(n + 1) * D_MODEL]))
        term = gate * _dot(b_ref[...], wb_ref[n])
        merged = term if merged is None else merged + term
    y = x + mod_ref[2:3, :] * _dot(merged.astype(BF16), wo_ref[...])
    o_ref[...] = y
    h2_ref[...] = _norm_mod(y, n2_ref[...], mod_ref[4:5, :], mod_ref[3:4, :]).astype(BF16)


def _merge(x, mod, norm1, norm2, oa, ob, oc, w_gate, w_branch, w_out, *, latent, seq):
    rows = x.shape[0]
    tm = min(TOK_TILE, seq)
    per_seq = seq // tm
    mod_map = (lambda t: (1 + t // per_seq, 0, 0)) if latent else (lambda t: (0, 0, 0))
    tile = lambda w: pl.BlockSpec((tm, w), lambda t: (t, 0))
    return pl.pallas_call(
        _merge_kernel, grid=(rows // tm,),
        in_specs=[tile(D_MODEL), pl.BlockSpec((None, MOD_CHUNKS, D_MODEL), mod_map), _whole(), _whole(),
                  tile(BRANCH_W), tile(BRANCH_W), tile(BRANCH_W), _whole(), _whole(), _whole()],
        out_specs=[tile(D_MODEL), tile(D_MODEL)],
        out_shape=[jax.ShapeDtypeStruct((rows, D_MODEL), F32), jax.ShapeDtypeStruct((rows, D_MODEL), BF16)],
        compiler_params=_params("parallel"), name="merge_lat" if latent else "merge_ctx",
    )(x, mod, norm1, norm2, oa, ob, oc, w_gate, w_branch, w_out)


def _ffn_tick(slot, h_ref, ua_ref, ug_ref, cwa_ref, cwg_ref, cba_ref, cbg_ref, dn_ref, o_ref, u_s, act_s, seq):
    rows = h_ref.shape[0]
    other = 1 - slot
    rc, pc, sub = FFN_CHUNK, FFN_PIECE, SUBLANES
    taps = [(cwa_ref[...], cba_ref[...]), (cwg_ref[...], cbg_ref[...])]
    halo = jnp.zeros((sub, FFN_COLS), F32)

    def conv(part, p0):
        w, b = taps[part]
        x = u_s[other, part, pl.ds(p0, pc), :]
        above = halo if p0 % seq == 0 else u_s[other, part, pl.ds(p0 - sub, sub), :]
        below = halo if (p0 + pc) % seq == 0 else u_s[other, part, pl.ds(p0 + pc, sub), :]
        xe = jnp.concatenate([above, x, below], axis=0)
        n = pc + 2 * sub
        prev = pltpu.roll(xe, 1, axis=0)[sub:sub + pc]
        nxt = pltpu.roll(xe, n - 1, axis=0)[sub:sub + pc]
        return prev * w[0:1, :] + x * w[1:2, :] + nxt * w[2:3, :] + b

    for r0 in range(0, rows, rc):
        rs = pl.ds(r0, rc)
        o_ref[rs, :] += _dot(act_s[slot, rs, :], dn_ref[...])
        for p0 in range(r0, r0 + rc, pc):
            a = conv(0, p0)
            g = conv(1, p0)
            act_s[other, pl.ds(p0, pc), :] = (g * _sigmoid(g) * a).astype(BF16)
        h = h_ref[rs, :]
        u_s[slot, 0, rs, :] = _dot(h, ua_ref[...])
        u_s[slot, 1, rs, :] = _dot(h, ug_ref[...])


def _ffn_kernel(h_ref, x_ref, mod_ref, fin_ref, ua_ref, ug_ref, cwa_ref, cwg_ref, cba_ref, cbg_ref, dn_ref,
                o_ref, u_s, act_s, *, seq, final, nj):
    t = pl.program_id(0)
    jc = jnp.maximum(t - 2, 0) % nj

    @pl.when(t == 0)
    def _():
        u_s[...] = jnp.zeros_like(u_s)
        act_s[...] = jnp.zeros_like(act_s)

    @pl.when(jc == 0)
    def _():
        o_ref[...] = jnp.zeros_like(o_ref)

    for slot in range(2):
        @pl.when(t % 2 == slot)
        def _(slot=slot):
            _ffn_tick(slot, h_ref, ua_ref, ug_ref, cwa_ref, cwg_ref, cba_ref, cbg_ref, dn_ref, o_ref, u_s, act_s, seq)

    @pl.when((jc == nj - 1) & (t >= 2))
    def _():
        y = x_ref[...] + mod_ref[5:6, :] * o_ref[...]
        if final:
            ms = jnp.mean(y * y, axis=-1, keepdims=True)
            y = y * lax.rsqrt(ms + EPS) * fin_ref[...]
        o_ref[...] = y


def _conv_ffn(h, x, mod, final_g, up, conv_w, conv_b, down, *, latent, seq, final):
    rows = x.shape[0]
    tm = FFN_ROWS
    nj = FFN_DIM // FFN_COLS
    total = (rows // tm) * nj
    ta = lambda t: jnp.minimum(t, total - 1)
    tb = lambda t: jnp.clip(t - 1, 0, total - 1)
    tc = lambda t: jnp.clip(t - 2, 0, total - 1)
    seqs_per_tile = max(seq // tm, 1)
    mod_map = (lambda t: (1 + (tc(t) // nj) // seqs_per_tile, 0, 0)) if latent else (lambda t: (0, 0, 0))
    col = lambda r, off, tick: pl.BlockSpec((r, FFN_COLS), lambda t: (0, off + tick(t) % nj))
    single = pl.Buffered(1)
    return pl.pallas_call(
        functools.partial(_ffn_kernel, seq=seq, final=final, nj=nj), grid=(total + 2,),
        in_specs=[pl.BlockSpec((tm, D_MODEL), lambda t: (ta(t) // nj, 0)),
                  pl.BlockSpec((tm, D_MODEL), lambda t: (tc(t) // nj, 0), pipeline_mode=single),
                  pl.BlockSpec((None, MOD_CHUNKS, D_MODEL), mod_map), _whole(),
                  col(D_MODEL, 0, ta), col(D_MODEL, nj, ta), col(3, 0, tb), col(3, nj, tb),
                  col(1, 0, tb), col(1, nj, tb),
                  pl.BlockSpec((FFN_COLS, D_MODEL), lambda t: (tc(t) % nj, 0))],
        out_specs=pl.BlockSpec((tm, D_MODEL), lambda t: (tc(t) // nj, 0)),
        out_shape=jax.ShapeDtypeStruct((rows, D_MODEL), F32),
        scratch_shapes=[pltpu.VMEM((2, 2, tm, FFN_COLS), F32), pltpu.VMEM((2, tm, FFN_COLS), BF16)],
        compiler_params=_params("arbitrary"), name="conv_ffn_lat" if latent else "conv_ffn_ctx",
    )(h, x, mod, final_g, up, up, conv_w, conv_w, conv_b, conv_b, down)


def _layer_weights(l, w_in, gqa_q_norm, gqa_k_norm, mla_q_norm, mla_kv_norm, mla_w_uq, mla_w_ukv):
    w = w_in[l]
    b = lambda a: a.astype(BF16)
    o = 0
    wq = w[:, o:o + 512]; o += 512
    wk = w[:, o:o + 128]; o += 128
    wv = w[:, o:o + 128]; o += 128
    wcq = w[:, o:o + MLA_Q_RANK]; o += MLA_Q_RANK
    wckv = w[:, o:o + MLA_KV_RANK]; o += MLA_KV_RANK
    wkpe = w[:, o:o + MLA_ROPE]; o += MLA_ROPE
    why = w[:, o:o + 3 * HY_CH]; o += 3 * HY_CH
    wgate = w[:, o:]
    dup = lambda a: jnp.concatenate([a[:, :64], a[:, :64], a[:, 64:], a[:, 64:]], axis=1)
    uq = mla_w_uq[l].reshape(MLA_Q_RANK, MLA_HEADS, MLA_NOPE + MLA_ROPE)
    wuq = jnp.concatenate([uq[:, :, :MLA_NOPE].reshape(MLA_Q_RANK, -1), uq[:, :, MLA_NOPE:].reshape(MLA_Q_RANK, -1)], axis=1)
    ukv = mla_w_ukv[l].reshape(MLA_KV_RANK, MLA_HEADS, MLA_NOPE + MLA_V)
    wukv = jnp.concatenate([ukv[:, :, :MLA_NOPE].reshape(MLA_KV_RANK, -1), ukv[:, :, MLA_NOPE:].reshape(MLA_KV_RANK, -1)], axis=1)
    wts = [b(wq), b(dup(wk)), b(dup(wv)), b(wcq), b(wckv), b(jnp.tile(wkpe, (1, 4))), b(why), b(wuq), b(wukv)]
    gains = [jnp.tile(gqa_q_norm[l], 8)[None, :], jnp.tile(gqa_k_norm[l], 4)[None, :],
             mla_q_norm[l][None, :], mla_kv_norm[l][None, :]]
    return wts, gains, b(wgate)


def kernel(x_prompt, x_sample, cache_gqa_k, cache_gqa_v, cache_mla_ckv, cache_mla_kpe, c, c_ctx, w_mod, b_mod, norm1, norm2, w_in, gqa_q_norm, gqa_k_norm, mla_q_norm, mla_kv_norm, mla_w_uq, mla_w_ukv, hy_short_w, hy_short_b, hy_w1, hy_b1, hy_w2, hy_b2, hy_w3, hy_freq, hy_bias, w_branch, w_out, ffn_up, ffn_conv_w, ffn_conv_b, ffn_down, final_norm):
    paths = {"ctx": dict(latent=False, batch=BATCH, seq=SEQ), "lat": dict(latent=True, batch=DEC_BATCH, seq=DEC_SEQ)}
    xs = {"ctx": x_prompt.reshape(BATCH * SEQ, D_MODEL), "lat": x_sample.reshape(DEC_BATCH * DEC_SEQ, D_MODEL)}

    cond = jnp.concatenate([c_ctx[None, :], c, jnp.zeros((COND_ROWS - 1 - DEC_BATCH, D_MODEL), F32)], axis=0)
    mod_all = _modulation(cond, w_mod, b_mod).reshape(DEPTH, COND_ROWS, MOD_CHUNKS, D_MODEL)

    hm_q, hm_k = _head_mean_matrix(512, HEAD_DIM), _head_mean_matrix(256, HEAD_DIM)
    rope = _rope_tables(DEC_SEQ, HEAD_DIM) + _rope_tables(DEC_SEQ, MLA_ROPE)
    dft = {name: _dft_matrices(p["seq"]) for name, p in paths.items()}
    hy_tab = {name: _hyena_tables(p["seq"]) for name, p in paths.items()}
    final_g = final_norm[None, :]

    new_k, new_v, new_ckv, new_kpe = [], [], [], []
    for l in range(DEPTH):
        mod = mod_all[l]
        wts, gains, w_gate = _layer_weights(l, w_in, gqa_q_norm, gqa_k_norm, mla_q_norm, mla_kv_norm, mla_w_uq, mla_w_ukv)
        consts = dict(gains=gains, hm_q=hm_q, hm_k=hm_k, rope=rope)
        n1, n2 = norm1[l][None, :], norm2[l][None, :]
        wb, wo = w_branch[l].astype(BF16), w_out[l].astype(BF16)
        up, down = ffn_up[l].astype(BF16), ffn_down[l].astype(BF16)
        w1p = jnp.zeros((LANES, HY_FFN), F32).at[:HY_EMB].set(hy_w1[l])

        ck = cache_gqa_k[:, l].reshape(DEC_BATCH * PAST_LEN, GQA_KV_HEADS, HEAD_DIM)
        cv = cache_gqa_v[:, l].reshape(DEC_BATCH * PAST_LEN, GQA_KV_HEADS, HEAD_DIM)
        dup = lambda a: jnp.concatenate([a[:, 0], a[:, 0], a[:, 1], a[:, 1]], axis=-1).astype(BF16)
        ck2, cv2 = dup(ck), dup(cv)
        cckv = cache_mla_ckv[:, l].reshape(DEC_BATCH * PAST_LEN, MLA_KV_RANK).astype(BF16)
        ckpe4 = jnp.tile(cache_mla_kpe[:, l].reshape(DEC_BATCH * PAST_LEN, MLA_ROPE), (1, 4)).astype(BF16)
        kcat_c, vb_c = _cache_kv(cckv, ckpe4, wts[8])

        for name, p in paths.items():
            latent, batch, seq = p["latent"], p["batch"], p["seq"]
            x = xs[name]
            outs = _in_projection(x, mod, n1, wts, consts, latent=latent, seq=seq)
            qa, k2, v2, qn, qpe, kcat, vb, hy = outs[:8]
            if not latent:
                new_k.append(outs[8]); new_v.append(outs[9]); new_ckv.append(outs[10]); new_kpe.append(outs[11])
            src_a = [(k2, v2, seq)]
            src_b = [(kcat, vb, seq)]
            if latent:
                src_a = [(ck2, cv2, PAST_LEN)] + src_a
                src_b = [(kcat_c, vb_c, PAST_LEN)] + src_b
            tq = min(ATTN_ROWS, seq)
            o_a = _attention(qa, None, src_a, batch=batch, seq=seq, mla=False, tq=tq)
            o_b = _attention(qn, qpe, src_b, batch=batch, seq=seq, mla=True, tq=tq)
            cmat, smat = dft[name]
            zemb, window = hy_tab[name]
            kre, kim = _hyena_filters(seq, zemb, window, w1p, hy_b1[l][None, :], hy_w2[l], hy_b2[l][None, :],
                                      hy_freq[l], hy_w3[l], cmat, smat)
            o_c = _hyena(hy, hy_short_w[l], hy_short_b[l][None, :], kre, kim, hy_bias[l], cmat, smat,
                         batch=batch, seq=seq)
            x, h2 = _merge(x, mod, n1, n2, o_a, o_b, o_c, w_gate, wb, wo, latent=latent, seq=seq)
            xs[name] = _conv_ffn(h2, x, mod, final_g, up, ffn_conv_w[l], ffn_conv_b[l][None, :], down,
                                 latent=latent, seq=seq, final=(l == DEPTH - 1))

    y_prompt = xs["ctx"].reshape(BATCH, SEQ, D_MODEL)
    y_sample = xs["lat"].reshape(DEC_BATCH, DEC_SEQ, D_MODEL)
    stack = lambda parts, shape: jnp.stack([a.reshape(shape) for a in parts], axis=1)
    return (y_prompt, y_sample,
            stack(new_k, (BATCH, SEQ, GQA_KV_HEADS, HEAD_DIM)), stack(new_v, (BATCH, SEQ, GQA_KV_HEADS, HEAD_DIM)),
            stack(new_ckv, (BATCH, SEQ, MLA_KV_RANK)), stack(new_kpe, (BATCH, SEQ, MLA_ROPE)))
```
